```python
import math
import jax, jax.numpy as jnp
from jax import lax
import numpy as np

D_MODEL = 2048
BATCH = 4
SEQ = 2048
DEPTH = 2

CHUNK = 64
N_META = 16
D_CONV = D_MODEL
CONV_A_WIDTH = 31
SSM_EXPAND = 2
D_INNER = SSM_EXPAND * D_MODEL
HEAD_DIM = 64
N_SSM_HEADS = D_INNER // HEAD_DIM
N_GROUPS = 8
D_STATE = 128
CONV_B_WIDTH = 4
D_XBC = D_INNER + 2 * N_GROUPS * D_STATE
D_FF = 5632
CONV_F_WIDTH = 3
P_IN = 2 * D_CONV + D_INNER + D_XBC + N_SSM_HEADS + 2 * D_MODEL
ALPHA = (2.0 * DEPTH) ** 0.25
BETA = (8.0 * DEPTH) ** -0.25
EPS = 1e-5

kernel_name = "hybrid_conformer_ssd_gated_deepnorm"


def layer_norm(x, g, b):
    xf = x.astype(jnp.float32)
    mu = jnp.mean(xf, axis=-1, keepdims=True)
    var = jnp.mean(jnp.square(xf - mu), axis=-1, keepdims=True)
    y = (xf - mu) * lax.rsqrt(var + EPS) * g.astype(jnp.float32) + b.astype(jnp.float32)
    return y.astype(x.dtype)


def causal_dwconv(x, w, b):
    K, C = w.shape
    out = lax.conv_general_dilated(
        x, w[:, None, :].astype(x.dtype), window_strides=(1,), padding=[(K - 1, 0)],
        dimension_numbers=("NWC", "WIO", "NWC"), feature_group_count=C)
    return out + b.astype(x.dtype)


def segsum_exp(a):
    q = a.shape[-1]
    cs = jnp.cumsum(a, axis=-1)
    diff = cs[..., :, None] - cs[..., None, :]
    mask = jnp.tril(jnp.ones((q, q), dtype=bool))
    return jnp.where(mask, jnp.exp(jnp.where(mask, diff, 0.0)), 0.0)


def ssd_chunked(xs, dt, a, bs, cs):
    b, L, H, P = xs.shape
    pad = (-L) % CHUNK
    R = H // N_GROUPS
    pw = ((0, 0), (pad, 0), (0, 0), (0, 0))
    xdt = jnp.pad(xs * dt[..., None].astype(xs.dtype), pw)
    da = jnp.pad(dt * a, ((0, 0), (pad, 0), (0, 0)))
    bs = jnp.pad(bs, pw)
    cs = jnp.pad(cs, pw)
    nc = (L + pad) // CHUNK
    xdt = xdt.reshape(b, nc, CHUNK, N_GROUPS, R, P)
    bs = bs.reshape(b, nc, CHUNK, N_GROUPS, D_STATE)
    cs = cs.reshape(b, nc, CHUNK, N_GROUPS, D_STATE)
    da = da.reshape(b, nc, CHUNK, N_GROUPS, R).transpose(0, 3, 4, 1, 2)
    da_cum = jnp.cumsum(da, axis=-1)

    decay = segsum_exp(da).astype(xs.dtype)
    cb = jnp.einsum("bcqgn,bckgn->bgcqk", cs, bs)
    m = cb[:, :, None] * decay
    y_diag = jnp.einsum("bgrcqk,bckgrp->bcqgrp", m, xdt)

    decay_to_end = jnp.exp(da_cum[..., -1:] - da_cum).transpose(0, 3, 4, 1, 2).astype(xs.dtype)
    chunk_states = jnp.einsum("bckgn,bckgrp->bcgrpn", bs, xdt * decay_to_end[..., None])

    chunk_decay = jnp.exp(da_cum[..., -1]).transpose(3, 0, 1, 2)

    def step(state, inp):
        dec, new = inp
        return state * dec[..., None, None] + new, state

    init = jnp.zeros((b, N_GROUPS, R, P, D_STATE), jnp.float32)
    _, states_in = lax.scan(step, init, (chunk_decay, chunk_states.astype(jnp.float32).transpose(1, 0, 2, 3, 4, 5)))
    states_in = states_in.transpose(1, 0, 2, 3, 4, 5).astype(xs.dtype)

    decay_from_start = jnp.exp(da_cum).transpose(0, 3, 4, 1, 2).astype(xs.dtype)
    y_off = jnp.einsum("bcqgn,bcgrpn->bcqgrp", cs, states_in) * decay_from_start[..., None]
    y = (y_diag + y_off).reshape(b, nc * CHUNK, H, P)
    return y[:, pad:]


def gated_group_rmsnorm(y, z, g):
    u = (y * jax.nn.silu(z)).astype(jnp.float32)
    u = u.reshape(*u.shape[:-1], N_GROUPS, D_INNER // N_GROUPS)
    u = u * lax.rsqrt(jnp.mean(jnp.square(u), axis=-1, keepdims=True) + EPS)
    return (u.reshape(y.shape) * g.astype(jnp.float32)).astype(y.dtype)


def mixer_block(h, w_in, conv_a_w, conv_a_b, ln_a_g, ln_a_b, w_a_out,
                conv_b_w, conv_b_b, dt_bias, a_log, d_skip, norm_b_g, w_b_out, w_o):
    b, L, _ = h.shape
    proj = h @ w_in
    s1 = 2 * D_CONV
    s2 = s1 + D_INNER
    s3 = s2 + D_XBC
    s4 = s3 + N_SSM_HEADS
    a_in, z, xbc, dt, gates = jnp.split(proj, [s1, s2, s3, s4], axis=-1)

    a = a_in[..., :D_CONV] * jax.nn.sigmoid(a_in[..., D_CONV:])
    a = causal_dwconv(a, conv_a_w, conv_a_b)
    a = jax.nn.silu(layer_norm(a, ln_a_g, ln_a_b))
    y_a = a @ w_a_out

    xbc = jax.nn.silu(causal_dwconv(xbc, conv_b_w, conv_b_b))
    xs, bs, cs = jnp.split(xbc, [D_INNER, D_INNER + N_GROUPS * D_STATE], axis=-1)
    xs = xs.reshape(b, L, N_SSM_HEADS, HEAD_DIM)
    bs = bs.reshape(b, L, N_GROUPS, D_STATE)
    cs = cs.reshape(b, L, N_GROUPS, D_STATE)
    dt = jax.nn.softplus((dt + dt_bias).astype(jnp.float32))
    a_neg = -jnp.exp(a_log.astype(jnp.float32))
    y = ssd_chunked(xs, dt, a_neg, bs, cs) + d_skip[:, None].astype(xs.dtype) * xs
    y = gated_group_rmsnorm(y.reshape(b, L, D_INNER), z, norm_b_g)
    y_b = y @ w_b_out

    g = jax.nn.sigmoid(gates)
    merged = g[..., :D_MODEL] * y_a + g[..., D_MODEL:] * y_b
    return merged @ w_o


def conv_ffn(h, w_up, conv_f_w, conv_f_b, w_down):
    u = causal_dwconv(h @ w_up, conv_f_w, conv_f_b)
    act = jax.nn.silu(u[..., :D_FF]) * u[..., D_FF:]
    return act @ w_down


def setup_inputs(seed: int = 0) -> dict:
    key = jax.random.key(seed)
    ks = iter(jax.random.split(key, 32))
    f32 = jnp.float32

    def nrm(shape, scale):
        return jax.random.normal(next(ks), shape, f32) * scale

    def gain(shape):
        return 1.0 + 0.05 * jax.random.normal(next(ks), shape, f32)

    def bias(shape):
        return 0.02 * jax.random.normal(next(ks), shape, f32)

    dt0 = jnp.exp(jax.random.uniform(next(ks), (DEPTH, N_SSM_HEADS), f32, math.log(1e-3), math.log(1e-1)))
    dt_bias = dt0 + jnp.log(-jnp.expm1(-dt0))
    a_log = jnp.log(jax.random.uniform(next(ks), (DEPTH, N_SSM_HEADS), f32, 1.0, 16.0))

    return {
        "x": nrm((BATCH, SEQ, D_MODEL), 1.0),
        "meta_tokens": nrm((N_META, D_MODEL), 1.0),
        "ln_in_g": gain((D_MODEL,)),
        "ln_in_b": bias((D_MODEL,)),
        "w_in": nrm((DEPTH, D_MODEL, P_IN), D_MODEL ** -0.5),
        "conv_a_w": nrm((DEPTH, CONV_A_WIDTH, D_CONV), CONV_A_WIDTH ** -0.5),
        "conv_a_b": bias((DEPTH, D_CONV)),
        "ln_a_g": gain((DEPTH, D_CONV)),
        "ln_a_b": bias((DEPTH, D_CONV)),
        "w_a_out": nrm((DEPTH, D_CONV, D_MODEL), D_CONV ** -0.5),
        "conv_b_w": nrm((DEPTH, CONV_B_WIDTH, D_XBC), CONV_B_WIDTH ** -0.5),
        "conv_b_b": bias((DEPTH, D_XBC)),
        "dt_bias": dt_bias,
        "a_log": a_log,
        "d_skip": gain((DEPTH, N_SSM_HEADS)),
        "norm_b_g": gain((DEPTH, D_INNER)),
        "w_b_out": nrm((DEPTH, D_INNER, D_MODEL), D_INNER ** -0.5),
        "w_o": nrm((DEPTH, D_MODEL, D_MODEL), BETA * D_MODEL ** -0.5),
        "ln1_g": gain((DEPTH, D_MODEL)),
        "ln1_b": bias((DEPTH, D_MODEL)),
        "w_up": nrm((DEPTH, D_MODEL, 2 * D_FF), D_MODEL ** -0.5),
        "conv_f_w": nrm((DEPTH, CONV_F_WIDTH, 2 * D_FF), CONV_F_WIDTH ** -0.5),
        "conv_f_b": bias((DEPTH, 2 * D_FF)),
        "w_down": nrm((DEPTH, D_FF, D_MODEL), BETA * D_FF ** -0.5),
        "ln2_g": gain((DEPTH, D_MODEL)),
        "ln2_b": bias((DEPTH, D_MODEL)),
    }


def reference(x, meta_tokens, ln_in_g, ln_in_b, w_in, conv_a_w, conv_a_b, ln_a_g, ln_a_b, w_a_out,
              conv_b_w, conv_b_b, dt_bias, a_log, d_skip, norm_b_g, w_b_out, w_o, ln1_g, ln1_b,
              w_up, conv_f_w, conv_f_b, w_down, ln2_g, ln2_b):
    b = x.shape[0]
    meta = jnp.broadcast_to(meta_tokens[None].astype(x.dtype), (b, N_META, D_MODEL))
    h = jnp.concatenate([meta, x], axis=1)
    h = layer_norm(h, ln_in_g, ln_in_b)
    for i in range(DEPTH):
        mix = mixer_block(h, w_in[i], conv_a_w[i], conv_a_b[i], ln_a_g[i], ln_a_b[i], w_a_out[i],
                          conv_b_w[i], conv_b_b[i], dt_bias[i], a_log[i], d_skip[i], norm_b_g[i],
                          w_b_out[i], w_o[i])
        h = layer_norm(ALPHA * h + mix, ln1_g[i], ln1_b[i])
        ffn = conv_ffn(h, w_up[i], conv_f_w[i], conv_f_b[i], w_down[i])
        h = layer_norm(ALPHA * h + ffn, ln2_g[i], ln2_b[i])
    return h[:, N_META:]
```

```python
import functools
import math

import jax
import jax.numpy as jnp
from jax import lax
from jax.experimental import pallas as pl
from jax.experimental.pallas import tpu as pltpu

F32 = jnp.float32
BF16 = jnp.bfloat16

D_MODEL = 2048
DEPTH = 2
CHUNK = 64
N_META = 16
D_CONV = D_MODEL
CONV_A_WIDTH = 31
D_INNER = 2 * D_MODEL
HEAD_DIM = 64
N_HEADS = D_INNER // HEAD_DIM
N_GROUPS = 8
HEADS_PER_GROUP = N_HEADS // N_GROUPS
GROUP_W = D_INNER // N_GROUPS
D_STATE = 128
CONV_B_WIDTH = 4
D_XBC = D_INNER + 2 * N_GROUPS * D_STATE
D_FF = 5632
CONV_F_WIDTH = 3
ALPHA = (2.0 * DEPTH) ** 0.25
EPS = 1e-5

OFF_Z = 2 * D_CONV
OFF_XBC = OFF_Z + D_INNER
OFF_DT = OFF_XBC + D_XBC
OFF_GATES = OFF_DT + N_HEADS

V7X_VMEM_BYTES = 64 * 1024 * 1024
VMEM_LIMIT = 56 * 1024 * 1024
LANES = 128
SUBLANES = 8


def _cparams(sem):
    return pltpu.CompilerParams(dimension_semantics=sem, vmem_limit_bytes=VMEM_LIMIT)


def _pad_row_mask(row0, rows, l_pad, n_pad):
    r = row0 + lax.broadcasted_iota(jnp.int32, (rows, 1), 0)
    return (r % l_pad) >= n_pad


def _layer_norm_rows(v, g, b):
    mu = jnp.mean(v, axis=-1, keepdims=True)
    d = v - mu
    var = jnp.mean(d * d, axis=-1, keepdims=True)
    return d * lax.rsqrt(var + EPS) * g + b


def _sigmoid(x):
    return 1.0 / (1.0 + jnp.exp(-x))


def _silu(x):
    return x * _sigmoid(x)


def _ln_in_kernel(x_ref, meta_ref, g_ref, b_ref, of_ref, ob_ref, *, blocks_per_batch, n_pad):
    j = pl.program_id(0) % blocks_per_batch
    g = g_ref[...]
    b = b_ref[...]

    @pl.when(j == 0)
    def _():
        m = _layer_norm_rows(meta_ref[...], g, b)
        y = jnp.concatenate([jnp.zeros((n_pad, D_MODEL), F32), m], axis=0)
        of_ref[...] = y
        ob_ref[...] = y.astype(BF16)

    @pl.when(j != 0)
    def _():
        y = _layer_norm_rows(x_ref[0], g, b)
        of_ref[...] = y
        ob_ref[...] = y.astype(BF16)


def _ln_in_call(x, meta, g, b, l_pad, n_pad):
    bsz, seq, d = x.shape
    bpb = l_pad // CHUNK
    xr = x.reshape(bsz * seq // CHUNK, CHUNK, d)
    spb = seq // CHUNK
    m = bsz * l_pad

    def x_map(n):
        return ((n // bpb) * spb + jnp.maximum(n % bpb - 1, 0), 0, 0)

    return pl.pallas_call(
        functools.partial(_ln_in_kernel, blocks_per_batch=bpb, n_pad=n_pad),
        grid=(bsz * bpb,),
        in_specs=[
            pl.BlockSpec((1, CHUNK, d), x_map),
            pl.BlockSpec((N_META, d), lambda n: (0, 0)),
            pl.BlockSpec((1, d), lambda n: (0, 0)),
            pl.BlockSpec((1, d), lambda n: (0, 0)),
        ],
        out_specs=[
            pl.BlockSpec((CHUNK, d), lambda n: (n, 0)),
            pl.BlockSpec((CHUNK, d), lambda n: (n, 0)),
        ],
        out_shape=[jax.ShapeDtypeStruct((m, d), F32), jax.ShapeDtypeStruct((m, d), BF16)],
        compiler_params=_cparams(("parallel",)),
        name="ln_in",
    )(xr, meta, g.reshape(1, d), b.reshape(1, d))


def _mm_kernel(x_ref, w_ref, o_ref):
    o_ref[...] = jnp.dot(x_ref[...], w_ref[...], preferred_element_type=F32).astype(o_ref.dtype)


def _mm_call(x, w, col0, n_out, tm, tn, out_dtype, name):
    m, k = x.shape
    assert col0 % tn == 0 and n_out % tn == 0 and m % tm == 0
    cb0 = col0 // tn
    return pl.pallas_call(
        _mm_kernel,
        grid=(m // tm, n_out // tn),
        in_specs=[
            pl.BlockSpec((tm, k), lambda i, j: (i, 0)),
            pl.BlockSpec((k, tn), lambda i, j: (0, cb0 + j)),
        ],
        out_specs=pl.BlockSpec((tm, tn), lambda i, j: (i, j)),
        out_shape=jax.ShapeDtypeStruct((m, n_out), out_dtype),
        compiler_params=_cparams(("parallel", "arbitrary")),
        name=name,
    )(x, w)


def _mm_glu_kernel(x_ref, w1_ref, w2_ref, o_ref):
    x = x_ref[...]
    a = jnp.dot(x, w1_ref[...], preferred_element_type=F32)
    b = jnp.dot(x, w2_ref[...], preferred_element_type=F32)
    o_ref[...] = (a * _sigmoid(b)).astype(o_ref.dtype)


def _mm_glu_call(x, w, n_half, tm, tn, out_dtype):
    m, k = x.shape
    nb = n_half // tn
    return pl.pallas_call(
        _mm_glu_kernel,
        grid=(m // tm, nb),
        in_specs=[
            pl.BlockSpec((tm, k), lambda i, j: (i, 0)),
            pl.BlockSpec((k, tn), lambda i, j: (0, j)),
            pl.BlockSpec((k, tn), lambda i, j: (0, nb + j)),
        ],
        out_specs=pl.BlockSpec((tm, tn), lambda i, j: (i, j)),
        out_shape=jax.ShapeDtypeStruct((m, n_half), out_dtype),
        compiler_params=_cparams(("parallel", "arbitrary")),
        name="mm_glu",
    )(x, w, w)


MIXA_HALO = 32
MIXA_RB = 64


def _mixa_kernel(x_ref, halo_ref, w_ref, cb_ref, g_ref, b_ref, o_ref, ext_ref, conv_ref, *, tm):
    ext_ref[0:MIXA_HALO, :] = halo_ref[...]
    ext_ref[MIXA_HALO:, :] = x_ref[...]
    n_strips = D_CONV // LANES
    lead = MIXA_HALO - (CONV_A_WIDTH - 1)

    def row_block(rb, carry):
        r0 = pl.multiple_of(rb * MIXA_RB, MIXA_RB)
        s1 = jnp.zeros((MIXA_RB, LANES), F32)
        for s in range(n_strips):
            cs = slice(s * LANES, (s + 1) * LANES)
            blk = ext_ref[pl.ds(r0, MIXA_RB + MIXA_HALO), cs]
            acc = jnp.zeros((MIXA_RB, LANES), F32) + cb_ref[:, cs]
            for j in range(SUBLANES):
                sh = blk if j == 0 else blk[j:j + MIXA_RB + MIXA_HALO - SUBLANES]
                for i in range(MIXA_HALO // SUBLANES + 1):
                    kk = SUBLANES * i + j - lead
                    if 0 <= kk < CONV_A_WIDTH:
                        acc = acc + w_ref[kk:kk + 1, cs] * sh[SUBLANES * i:SUBLANES * i + MIXA_RB]
            conv_ref[:, cs] = acc
            s1 = s1 + acc
        mu = jnp.sum(s1, axis=1, keepdims=True) * (1.0 / D_CONV)
        s2 = jnp.zeros((MIXA_RB, LANES), F32)
        for s in range(n_strips):
            cs = slice(s * LANES, (s + 1) * LANES)
            d = conv_ref[:, cs] - mu
            s2 = s2 + d * d
        inv = lax.rsqrt(jnp.sum(s2, axis=1, keepdims=True) * (1.0 / D_CONV) + EPS)
        for s in range(n_strips):
            cs = slice(s * LANES, (s + 1) * LANES)
            y = (conv_ref[:, cs] - mu) * inv * g_ref[:, cs] + b_ref[:, cs]
            o_ref[pl.ds(r0, MIXA_RB), cs] = _silu(y).astype(o_ref.dtype)
        return carry

    lax.fori_loop(0, tm // MIXA_RB, row_block, 0)


def _mixa_call(a, conv_w, conv_b, ln_g, ln_b, tm):
    m, c = a.shape
    hb = tm // MIXA_HALO
    return pl.pallas_call(
        functools.partial(_mixa_kernel, tm=tm),
        grid=(m // tm,),
        in_specs=[
            pl.BlockSpec((tm, c), lambda i: (i, 0)),
            pl.BlockSpec((MIXA_HALO, c), lambda i: (jnp.maximum(i * hb - 1, 0), 0)),
            pl.BlockSpec((CONV_A_WIDTH, c), lambda i: (0, 0)),
            pl.BlockSpec((1, c), lambda i: (0, 0)),
            pl.BlockSpec((1, c), lambda i: (0, 0)),
            pl.BlockSpec((1, c), lambda i: (0, 0)),
        ],
        out_specs=pl.BlockSpec((tm, c), lambda i: (i, 0)),
        out_shape=jax.ShapeDtypeStruct((m, c), BF16),
        scratch_shapes=[
            pltpu.VMEM((tm + MIXA_HALO, c), F32),
            pltpu.VMEM((MIXA_RB, c), F32),
        ],
        compiler_params=_cparams(("parallel",)),
        name="mixer_a",
    )(a, a, conv_w, conv_b.reshape(1, c), ln_g.reshape(1, c), ln_b.reshape(1, c))


SSD_HALO = 8


def _split3(x):
    x1 = x.astype(BF16)
    r1 = x - x1.astype(F32)
    x2 = r1.astype(BF16)
    r2 = r1 - x2.astype(F32)
    return x1, x2, r2.astype(BF16)


def _dot01(lhs01, x):
    p1, p2, p3 = _split3(x)
    out = jnp.dot(lhs01, p1, preferred_element_type=F32)
    out = out + jnp.dot(lhs01, p2, preferred_element_type=F32)
    return out + jnp.dot(lhs01, p3, preferred_element_type=F32)


def _dot01_r(x, rhs01):
    p1, p2, p3 = _split3(x)
    out = jnp.dot(p1, rhs01, preferred_element_type=F32)
    out = out + jnp.dot(p2, rhs01, preferred_element_type=F32)
    return out + jnp.dot(p3, rhs01, preferred_element_type=F32)


def _conv4_silu(x_ref, halo_ref, w_ref, b_ref, dst_ref, tl):
    ext = jnp.concatenate([halo_ref[...], x_ref[...]], axis=0)
    acc = b_ref[...] + w_ref[0:1, :] * ext[SSD_HALO - 3:SSD_HALO - 3 + tl]
    for k in range(1, CONV_B_WIDTH):
        o = SSD_HALO - (CONV_B_WIDTH - 1) + k
        acc = acc + w_ref[k:k + 1, :] * ext[o:o + tl]
    dst_ref[...] = _silu(acc)


def _ssd_kernel(z_ref, xs_ref, bs_ref, cs_ref, xsh_ref, bsh_ref, csh_ref, dt_ref,
                wx_ref, wb_ref, wc_ref, bx_ref, bb_ref, bc_ref,
                dtb_ref, alog_ref, dskip_ref, ng_ref,
                o_ref,
                xs_s, bs_s, cs_s, da_s, dts_s, state_s, *, tl, n_pad):
    g = pl.program_id(1)
    s = pl.program_id(2)
    q = CHUNK
    gw = GROUP_W

    _conv4_silu(xs_ref, xsh_ref, wx_ref, bx_ref, xs_s, tl)
    _conv4_silu(bs_ref, bsh_ref, wb_ref, bb_ref, bs_s, tl)
    _conv4_silu(cs_ref, csh_ref, wc_ref, bc_ref, cs_s, tl)

    xdt = dt_ref[...] + dtb_ref[...]
    dt = jnp.maximum(xdt, 0.0) + jnp.log(1.0 + jnp.exp(-jnp.abs(xdt)))
    real = _pad_row_mask(s * tl, tl, 1 << 30, n_pad)
    dt = jnp.where(real, dt, 0.0)
    dts_s[...] = dt
    da_s[...] = dt * (-jnp.exp(alog_ref[...]))

    @pl.when(s == 0)
    def _():
        state_s[...] = jnp.zeros_like(state_s)

    ri = lax.broadcasted_iota(jnp.int32, (q, gw), 0)
    ci = lax.broadcasted_iota(jnp.int32, (q, gw), 1)
    tri = ri >= (ci % q)
    eye = ri == (ci % q)
    ltri = (lax.broadcasted_iota(jnp.int32, (q, q), 0)
            >= lax.broadcasted_iota(jnp.int32, (q, q), 1)).astype(BF16)
    ones_qq = jnp.ones((q, q), BF16)
    hi = lax.broadcasted_iota(jnp.int32, (LANES, gw), 0)
    hj = lax.broadcasted_iota(jnp.int32, (LANES, gw), 1)
    sel = (hi == g * HEADS_PER_GROUP + hj // HEAD_DIM).astype(BF16)
    half = gw // 2
    bi = lax.broadcasted_iota(jnp.int32, (half, half), 0)
    bj = lax.broadcasted_iota(jnp.int32, (half, half), 1)
    bdiag = (bi // q) == (bj // HEAD_DIM)

    def chunk(c, carry):
        r0 = pl.multiple_of(c * q, q)
        rows = pl.ds(r0, q)
        xs_c = xs_s[rows, :]
        bs_c = bs_s[rows, :]
        cs_c = cs_s[rows, :]
        cum = _dot01(ltri, da_s[rows, :])
        both = _dot01_r(jnp.concatenate([dts_s[rows, :], cum], axis=0), sel)
        dtb = both[:q]
        ce = both[q:]
        rr = _dot01(ones_qq, jnp.where(eye, ce, 0.0))
        decay = jnp.where(tri, jnp.exp(jnp.where(tri, ce - rr, 0.0)), 0.0)

        cs_b = cs_c.astype(BF16)
        bs_b = bs_c.astype(BF16)
        bs_rep = jnp.concatenate([bs_b] * HEADS_PER_GROUP, axis=0)
        cbt = lax.dot_general(cs_b, bs_rep, (((1,), (1,)), ((), ())),
                              preferred_element_type=F32)
        mm = (cbt * decay).astype(BF16)
        xdt_c = xs_c * dtb
        xdt_b = xdt_c.astype(BF16)
        ys = []
        for hh in range(2):
            blk = xdt_b[:, hh * half:(hh + 1) * half]
            xd = jnp.where(bdiag, jnp.concatenate([blk] * (half // q), axis=0), jnp.zeros((), BF16))
            ys.append(jnp.dot(mm[:, hh * half:(hh + 1) * half], xd, preferred_element_type=F32))
        y = jnp.concatenate(ys, axis=1)

        st = state_s[...]
        y = y + jnp.dot(cs_b, st.astype(BF16), preferred_element_type=F32) * jnp.exp(ce)
        last = ce[q - 1:q, :]
        xw = (xdt_c * jnp.exp(last - ce)).astype(BF16)
        bs_t = bs_c.T.astype(BF16)
        state_s[...] = st * jnp.exp(last) + jnp.dot(bs_t, xw, preferred_element_type=F32)

        y = y + dskip_ref[...] * xs_c
        u = y * _silu(z_ref[rows, :])
        ms = jnp.mean(u * u, axis=-1, keepdims=True)
        o_ref[rows, :] = (u * lax.rsqrt(ms + EPS) * ng_ref[...]).astype(o_ref.dtype)
        return carry

    lax.fori_loop(0, tl // q, chunk, 0)


def _ssd_call(zx, dtp, conv_w, conv_b, dt_bias, a_log, d_skip, norm_g, bsz, l_pad, n_pad, tl):
    m = zx.shape[0]
    spb = l_pad // tl
    hpt = tl // SSD_HALO
    gw = GROUP_W
    n = D_STATE
    zb = 0
    xb = D_INNER // gw
    bb = (D_INNER + D_INNER) // n
    cb = bb + N_GROUPS
    wxb = 0
    wbb = D_INNER // n
    wcb = wbb + N_GROUPS

    def row(b, g, s):
        return b * spb + s

    def halo(b, g, s):
        return jnp.maximum((b * spb + s) * hpt - 1, 0)

    pad128 = lambda v: jnp.pad(v.astype(F32), (0, LANES - v.shape[0])).reshape(1, LANES)
    cb2 = conv_b.reshape(1, D_XBC)
    in_specs = [
        pl.BlockSpec((tl, gw), lambda b, g, s: (row(b, g, s), zb + g)),
        pl.BlockSpec((tl, gw), lambda b, g, s: (row(b, g, s), xb + g)),
        pl.BlockSpec((tl, n), lambda b, g, s: (row(b, g, s), bb + g)),
        pl.BlockSpec((tl, n), lambda b, g, s: (row(b, g, s), cb + g)),
        pl.BlockSpec((SSD_HALO, gw), lambda b, g, s: (halo(b, g, s), xb + g)),
        pl.BlockSpec((SSD_HALO, n), lambda b, g, s: (halo(b, g, s), bb + g)),
        pl.BlockSpec((SSD_HALO, n), lambda b, g, s: (halo(b, g, s), cb + g)),
        pl.BlockSpec((tl, LANES), lambda b, g, s: (row(b, g, s), 0)),
        pl.BlockSpec((CONV_B_WIDTH, gw), lambda b, g, s: (0, wxb + g)),
        pl.BlockSpec((CONV_B_WIDTH, n), lambda b, g, s: (0, wbb + g)),
        pl.BlockSpec((CONV_B_WIDTH, n), lambda b, g, s: (0, wcb + g)),
        pl.BlockSpec((1, gw), lambda b, g, s: (0, wxb + g)),
        pl.BlockSpec((1, n), lambda b, g, s: (0, wbb + g)),
        pl.BlockSpec((1, n), lambda b, g, s: (0, wcb + g)),
        pl.BlockSpec((1, LANES), lambda b, g, s: (0, 0)),
        pl.BlockSpec((1, LANES), lambda b, g, s: (0, 0)),
        pl.BlockSpec((1, gw), lambda b, g, s: (0, g)),
        pl.BlockSpec((1, gw), lambda b, g, s: (0, g)),
    ]
    return pl.pallas_call(
        functools.partial(_ssd_kernel, tl=tl, n_pad=n_pad),
        grid=(bsz, N_GROUPS, spb),
        in_specs=in_specs,
        out_specs=pl.BlockSpec((tl, gw), lambda b, g, s: (row(b, g, s), g)),
        out_shape=jax.ShapeDtypeStruct((m, D_INNER), BF16),
        scratch_shapes=[
            pltpu.VMEM((tl, gw), F32),
            pltpu.VMEM((tl, n), F32),
            pltpu.VMEM((tl, n), F32),
            pltpu.VMEM((tl, LANES), F32),
            pltpu.VMEM((tl, LANES), F32),
            pltpu.VMEM((n, gw), F32),
        ],
        compiler_params=_cparams(("parallel", "parallel", "arbitrary")),
        name="ssd",
    )(zx, zx, zx, zx, zx, zx, zx, dtp,
      conv_w, conv_w, conv_w, cb2, cb2, cb2,
      pad128(dt_bias), pad128(a_log),
      jnp.repeat(d_skip.astype(F32), HEAD_DIM).reshape(1, D_INNER),
      norm_g.reshape(1, D_INNER))


def _merge_kernel(a_ref, y_ref, ga_ref, gb_ref, wa_ref, wb_ref, o_ref):
    ya = jnp.dot(a_ref[...], wa_ref[...], preferred_element_type=F32)
    yb = jnp.dot(y_ref[...], wb_ref[...], preferred_element_type=F32)
    o_ref[...] = (_sigmoid(ga_ref[...]) * ya + _sigmoid(gb_ref[...]) * yb).astype(o_ref.dtype)


def _merge_call(a_act, y_norm, gates, w_a, w_b, tm, tn):
    m = a_act.shape[0]
    nb = D_MODEL // tn
    return pl.pallas_call(
        _merge_kernel,
        grid=(m // tm, nb),
        in_specs=[
            pl.BlockSpec((tm, D_CONV), lambda i, j: (i, 0)),
            pl.BlockSpec((tm, D_INNER), lambda i, j: (i, 0)),
            pl.BlockSpec((tm, tn), lambda i, j: (i, j)),
            pl.BlockSpec((tm, tn), lambda i, j: (i, nb + j)),
            pl.BlockSpec((D_CONV, tn), lambda i, j: (0, j)),
            pl.BlockSpec((D_INNER, tn), lambda i, j: (0, j)),
        ],
        out_specs=pl.BlockSpec((tm, tn), lambda i, j: (i, j)),
        out_shape=jax.ShapeDtypeStruct((m, D_MODEL), BF16),
        compiler_params=_cparams(("parallel", "arbitrary")),
        name="merge",
    )(a_act, y_norm, gates, gates, w_a, w_b)


def _mm_ln_kernel(x_ref, w_ref, res_ref, g_ref, b_ref, of_ref, ob_ref, acc_ref, *, nk, tm, l_pad, n_pad):
    k = pl.program_id(1)

    @pl.when(k == 0)
    def _():
        acc_ref[...] = jnp.zeros_like(acc_ref)

    acc_ref[...] += jnp.dot(x_ref[...], w_ref[...], preferred_element_type=F32)

    @pl.when(k == nk - 1)
    def _():
        v = ALPHA * res_ref[...] + acc_ref[...]
        y = _layer_norm_rows(v, g_ref[...], b_ref[...])
        real = _pad_row_mask(pl.program_id(0) * tm, tm, l_pad, n_pad)
        y = jnp.where(real, y, 0.0)
        of_ref[...] = y
        ob_ref[...] = y.astype(BF16)


def _mm_ln_call(x, w, res, g, b, tm, tk, l_pad, n_pad, name):
    m, kdim = x.shape
    d = w.shape[1]
    nk = kdim // tk
    return pl.pallas_call(
        functools.partial(_mm_ln_kernel, nk=nk, tm=tm, l_pad=l_pad, n_pad=n_pad),
        grid=(m // tm, nk),
        in_specs=[
            pl.BlockSpec((tm, tk), lambda i, k: (i, k)),
            pl.BlockSpec((tk, d), lambda i, k: (k, 0)),
            pl.BlockSpec((tm, d), lambda i, k: (i, 0)),
            pl.BlockSpec((1, d), lambda i, k: (0, 0)),
            pl.BlockSpec((1, d), lambda i, k: (0, 0)),
        ],
        out_specs=[
            pl.BlockSpec((tm, d), lambda i, k: (i, 0)),
            pl.BlockSpec((tm, d), lambda i, k: (i, 0)),
        ],
        out_shape=[jax.ShapeDtypeStruct((m, d), F32), jax.ShapeDtypeStruct((m, d), BF16)],
        scratch_shapes=[pltpu.VMEM((tm, d), F32)],
        compiler_params=_cparams(("parallel", "arbitrary")),
        name=name,
    )(x, w, res, g.reshape(1, d), b.reshape(1, d))


FFN_HALO = 16


def _ffn_up_kernel(x_ref, xh_ref, w1_ref, w2_ref, cw1_ref, cw2_ref, cb1_ref, cb2_ref, o_ref, *, tm):
    x = jnp.concatenate([xh_ref[...], x_ref[...]], axis=0)

    def conv(w_ref, cw_ref, cb_ref):
        u = jnp.dot(x, w_ref[...], preferred_element_type=F32)
        acc = cb_ref[...] + cw_ref[CONV_F_WIDTH - 1:CONV_F_WIDTH, :] * u[FFN_HALO:]
        for k in range(CONV_F_WIDTH - 1):
            o = FFN_HALO - (CONV_F_WIDTH - 1) + k
            acc = acc + cw_ref[k:k + 1, :] * u[o:o + tm]
        return acc

    u1 = conv(w1_ref, cw1_ref, cb1_ref)
    u2 = conv(w2_ref, cw2_ref, cb2_ref)
    o_ref[...] = (_silu(u1) * u2).astype(o_ref.dtype)


def _ffn_up_call(hb, w_up, conv_w, conv_b, tm, tn):
    m, k = hb.shape
    nb = D_FF // tn
    hpt = tm // FFN_HALO
    cb = conv_b.reshape(1, 2 * D_FF)
    return pl.pallas_call(
        functools.partial(_ffn_up_kernel, tm=tm),
        grid=(m // tm, nb),
        in_specs=[
            pl.BlockSpec((tm, k), lambda i, j: (i, 0)),
            pl.BlockSpec((FFN_HALO, k), lambda i, j: (jnp.maximum(i * hpt - 1, 0), 0)),
            pl.BlockSpec((k, tn), lambda i, j: (0, j)),
            pl.BlockSpec((k, tn), lambda i, j: (0, nb + j)),
            pl.BlockSpec((CONV_F_WIDTH, tn), lambda i, j: (0, j)),
            pl.BlockSpec((CONV_F_WIDTH, tn), lambda i, j: (0, nb + j)),
            pl.BlockSpec((1, tn), lambda i, j: (0, j)),
            pl.BlockSpec((1, tn), lambda i, j: (0, nb + j)),
        ],
        out_specs=pl.BlockSpec((tm, tn), lambda i, j: (i, j)),
        out_shape=jax.ShapeDtypeStruct((m, D_FF), BF16),
        compiler_params=_cparams(("parallel", "arbitrary")),
        name="ffn_up",
    )(hb, hb, w_up, w_up, conv_w, conv_w, cb, cb)


def kernel(x, meta_tokens, ln_in_g, ln_in_b, w_in, conv_a_w, conv_a_b, ln_a_g, ln_a_b, w_a_out,
           conv_b_w, conv_b_b, dt_bias, a_log, d_skip, norm_b_g, w_b_out, w_o, ln1_g, ln1_b,
           w_up, conv_f_w, conv_f_b, w_down, ln2_g, ln2_b):
    bsz, seq, d = x.shape
    assert d == D_MODEL and seq % CHUNK == 0 and N_META <= CHUNK
    n_pad = CHUNK - N_META
    l_pad = seq + CHUNK
    m = bsz * l_pad

    tm = 1056
    tl = 704
    tr = 352
    assert m % tm == 0 and l_pad % tl == 0 and tl % CHUNK == 0 and m % tr == 0

    hf, hb = _ln_in_call(x, meta_tokens, ln_in_g, ln_in_b, l_pad, n_pad)

    for i in range(DEPTH):
        w_in_b = w_in[i].astype(BF16)
        w_g_b = w_in[i][:, OFF_GATES:].astype(BF16)

        a_glu = _mm_glu_call(hb, w_in_b, D_CONV, tm, 512, F32)
        zx = _mm_call(hb, w_in_b, OFF_Z, D_INNER + D_XBC, tm, 1024, F32, "mm_zx")
        dtp = _mm_call(hb, w_in_b, OFF_DT, LANES, tm, LANES, F32, "mm_dt")
        gates = _mm_call(hb, w_g_b, 0, 2 * D_MODEL, tm, 1024, F32, "mm_gates")

        a_act = _mixa_call(a_glu, conv_a_w[i], conv_a_b[i], ln_a_g[i], ln_a_b[i], tl)
        y_norm = _ssd_call(zx, dtp, conv_b_w[i], conv_b_b[i], dt_bias[i], a_log[i], d_skip[i],
                           norm_b_g[i], bsz, l_pad, n_pad, tl)

        merged = _merge_call(a_act, y_norm, gates, w_a_out[i].astype(BF16), w_b_out[i].astype(BF16),
                             tl, 512)
        hf, hb = _mm_ln_call(merged, w_o[i].astype(BF16), hf, ln1_g[i], ln1_b[i],
                             tr, 1024, l_pad, n_pad, "mm_o_ln")

        act = _ffn_up_call(hb, w_up[i].astype(BF16), conv_f_w[i], conv_f_b[i], tm, 512)
        hf, hb = _mm_ln_call(act, w_down[i].astype(BF16), hf, ln2_g[i], ln2_b[i],
                             tr, 512, l_pad, n_pad, "mm_down_ln")

    return hf.reshape(bsz, l_pad, d)[:, CHUNK:, :]
```

```python
import functools

import jax
import jax.numpy as jnp
from jax import lax
from jax.experimental import pallas as pl
from jax.experimental.pallas import tpu as pltpu

F32 = jnp.float32
BF16 = jnp.bfloat16

D_MODEL = 2048
DEPTH = 2
CHUNK = 64
N_META = 16
D_CONV = D_MODEL
CONV_A_WIDTH = 31
D_INNER = 2 * D_MODEL
HEAD_DIM = 64
N_HEADS = D_INNER // HEAD_DIM
N_GROUPS = 8
HEADS_PER_GROUP = N_HEADS // N_GROUPS
GROUP_W = D_INNER // N_GROUPS
D_STATE = 128
CONV_B_WIDTH = 4
D_XBC = D_INNER + 2 * N_GROUPS * D_STATE
D_FF = 5632
CONV_F_WIDTH = 3
ALPHA = (2.0 * DEPTH) ** 0.25
EPS = 1e-5

OFF_Z = 2 * D_CONV
OFF_XBC = OFF_Z + D_INNER
OFF_BC = OFF_XBC + D_INNER
OFF_DT = OFF_XBC + D_XBC
OFF_GATES = OFF_DT + N_HEADS

VMEM_LIMIT = 56 * 1024 * 1024
LANES = 128
SUBLANES = 8
BF16_ROWS = 16


def _cparams(sem):
    return pltpu.CompilerParams(dimension_semantics=sem, vmem_limit_bytes=VMEM_LIMIT)


def _pad_row_mask(row0, rows, l_pad, n_pad):
    r = row0 + lax.broadcasted_iota(jnp.int32, (rows, 1), 0)
    return (r % l_pad) >= n_pad


def _layer_norm_rows(v, g, b):
    mu = jnp.mean(v, axis=-1, keepdims=True)
    d = v - mu
    var = jnp.mean(d * d, axis=-1, keepdims=True)
    return d * lax.rsqrt(var + EPS) * g + b


def _sigmoid(x):
    return 1.0 / (1.0 + jnp.exp(-x))


def _silu(x):
    return x * _sigmoid(x)


def _split3(x):
    x1 = x.astype(BF16)
    r1 = x - x1.astype(F32)
    x2 = r1.astype(BF16)
    r2 = r1 - x2.astype(F32)
    return x1, x2, r2.astype(BF16)


def _dot01(lhs01, x):
    p1, p2, p3 = _split3(x)
    out = jnp.dot(lhs01, p1, preferred_element_type=F32)
    out = out + jnp.dot(lhs01, p2, preferred_element_type=F32)
    return out + jnp.dot(lhs01, p3, preferred_element_type=F32)


def _dot01_r(x, rhs01):
    p1, p2, p3 = _split3(x)
    out = jnp.dot(p1, rhs01, preferred_element_type=F32)
    out = out + jnp.dot(p2, rhs01, preferred_element_type=F32)
    return out + jnp.dot(p3, rhs01, preferred_element_type=F32)


def _causal_taps(u_halo, u_main, cw_ref, cb_ref, width, tm):
    ext = jnp.concatenate([u_halo[BF16_ROWS - SUBLANES:], u_main[:SUBLANES]], axis=0)
    head = cb_ref[...] + cw_ref[width - 1:width, :] * ext[SUBLANES:]
    tail = cb_ref[...] + cw_ref[width - 1:width, :] * u_main[SUBLANES:]
    for k in range(width - 1):
        back = width - 1 - k
        head = head + cw_ref[k:k + 1, :] * ext[SUBLANES - back:2 * SUBLANES - back]
        tail = tail + cw_ref[k:k + 1, :] * u_main[SUBLANES - back:tm - back]
    return jnp.concatenate([head, tail], axis=0)


def _ln_in_kernel(x_ref, meta_ref, g_ref, b_ref, of_ref, ob_ref, *, blocks_per_batch, n_pad):
    j = pl.program_id(0) % blocks_per_batch
    g = g_ref[...]
    b = b_ref[...]

    @pl.when(j == 0)
    def _():
        m = _layer_norm_rows(meta_ref[...], g, b)
        y = jnp.concatenate([jnp.zeros((n_pad, D_MODEL), F32), m], axis=0)
        of_ref[...] = y
        ob_ref[...] = y.astype(BF16)

    @pl.when(j != 0)
    def _():
        y = _layer_norm_rows(x_ref[0], g, b)
        of_ref[...] = y
        ob_ref[...] = y.astype(BF16)


def _ln_in_call(x, meta, g, b, l_pad, n_pad):
    bsz, seq, d = x.shape
    bpb = l_pad // CHUNK
    xr = x.reshape(bsz * seq // CHUNK, CHUNK, d)
    spb = seq // CHUNK
    m = bsz * l_pad

    def x_map(n):
        return ((n // bpb) * spb + jnp.maximum(n % bpb - 1, 0), 0, 0)

    return pl.pallas_call(
        functools.partial(_ln_in_kernel, blocks_per_batch=bpb, n_pad=n_pad),
        grid=(bsz * bpb,),
        in_specs=[
            pl.BlockSpec((1, CHUNK, d), x_map),
            pl.BlockSpec((N_META, d), lambda n: (0, 0)),
            pl.BlockSpec((1, d), lambda n: (0, 0)),
            pl.BlockSpec((1, d), lambda n: (0, 0)),
        ],
        out_specs=[
            pl.BlockSpec((CHUNK, d), lambda n: (n, 0)),
            pl.BlockSpec((CHUNK, d), lambda n: (n, 0)),
        ],
        out_shape=[jax.ShapeDtypeStruct((m, d), F32), jax.ShapeDtypeStruct((m, d), BF16)],
        compiler_params=_cparams(("parallel",)),
        name="ln_in",
    )(xr, meta, g.reshape(1, d), b.reshape(1, d))


def _mm_kernel(x_ref, w_ref, o_ref):
    o_ref[...] = jnp.dot(x_ref[...], w_ref[...], preferred_element_type=F32).astype(o_ref.dtype)


def _mm_call(x, w, col0, n_out, tm, tn, out_dtype, name):
    m, k = x.shape
    assert col0 % tn == 0 and n_out % tn == 0 and m % tm == 0
    cb0 = col0 // tn
    return pl.pallas_call(
        _mm_kernel,
        grid=(m // tm, n_out // tn),
        in_specs=[
            pl.BlockSpec((tm, k), lambda i, j: (i, 0)),
            pl.BlockSpec((k, tn), lambda i, j: (0, cb0 + j)),
        ],
        out_specs=pl.BlockSpec((tm, tn), lambda i, j: (i, j)),
        out_shape=jax.ShapeDtypeStruct((m, n_out), out_dtype),
        compiler_params=_cparams(("parallel", "arbitrary")),
        name=name,
    )(x, w)


def _mm_glu_kernel(x_ref, w1_ref, w2_ref, o_ref):
    x = x_ref[...]
    a = jnp.dot(x, w1_ref[...], preferred_element_type=F32)
    b = jnp.dot(x, w2_ref[...], preferred_element_type=F32)
    o_ref[...] = (a * _sigmoid(b)).astype(o_ref.dtype)


def _mm_glu_call(x, w, n_half, tm, tn, out_dtype):
    m, k = x.shape
    nb = n_half // tn
    return pl.pallas_call(
        _mm_glu_kernel,
        grid=(m // tm, nb),
        in_specs=[
            pl.BlockSpec((tm, k), lambda i, j: (i, 0)),
            pl.BlockSpec((k, tn), lambda i, j: (0, j)),
            pl.BlockSpec((k, tn), lambda i, j: (0, nb + j)),
        ],
        out_specs=pl.BlockSpec((tm, tn), lambda i, j: (i, j)),
        out_shape=jax.ShapeDtypeStruct((m, n_half), out_dtype),
        compiler_params=_cparams(("parallel", "arbitrary")),
        name="mm_glu",
    )(x, w, w)


def _mm_conv_kernel(x_ref, xh_ref, w_ref, cw_ref, cb_ref, o_ref, *, tm):
    w = w_ref[...]
    u_main = jnp.dot(x_ref[...], w, preferred_element_type=F32)
    u_halo = jnp.dot(xh_ref[...], w, preferred_element_type=F32)
    o_ref[...] = _silu(_causal_taps(u_halo, u_main, cw_ref, cb_ref, CONV_B_WIDTH, tm)).astype(o_ref.dtype)


def _mm_conv_call(x, w, col0, n_out, conv_w, conv_b, conv_col0, tm, tn, out_dtype, name):
    m, k = x.shape
    assert col0 % tn == 0 and n_out % tn == 0 and m % tm == 0 and conv_col0 % tn == 0
    cb0 = col0 // tn
    cc0 = conv_col0 // tn
    hpt = tm // BF16_ROWS
    cb = conv_b.reshape(1, -1)
    return pl.pallas_call(
        functools.partial(_mm_conv_kernel, tm=tm),
        grid=(m // tm, n_out // tn),
        in_specs=[
            pl.BlockSpec((tm, k), lambda i, j: (i, 0)),
            pl.BlockSpec((BF16_ROWS, k), lambda i, j: (jnp.maximum(i * hpt - 1, 0), 0)),
            pl.BlockSpec((k, tn), lambda i, j: (0, cb0 + j)),
            pl.BlockSpec((CONV_B_WIDTH, tn), lambda i, j: (0, cc0 + j)),
            pl.BlockSpec((1, tn), lambda i, j: (0, cc0 + j)),
        ],
        out_specs=pl.BlockSpec((tm, tn), lambda i, j: (i, j)),
        out_shape=jax.ShapeDtypeStruct((m, n_out), out_dtype),
        compiler_params=_cparams(("parallel", "arbitrary")),
        name=name,
    )(x, x, w, conv_w, cb)


def _mm_dt_kernel(x_ref, w_ref, dtb_ref, alog_ref, dt_ref, cum_ref, *, tm, l_pad, n_pad):
    raw = jnp.dot(x_ref[...], w_ref[...], preferred_element_type=F32) + dtb_ref[...]
    dt = jnp.maximum(raw, 0.0) + jnp.log(1.0 + jnp.exp(-jnp.abs(raw)))
    real = _pad_row_mask(pl.program_id(0) * tm, tm, l_pad, n_pad)
    dt = jnp.where(real, dt, 0.0)
    dt_ref[...] = dt
    da = dt * (-jnp.exp(alog_ref[...]))
    ltri = (lax.broadcasted_iota(jnp.int32, (CHUNK, CHUNK), 0)
            >= lax.broadcasted_iota(jnp.int32, (CHUNK, CHUNK), 1)).astype(BF16)
    for c in range(tm // CHUNK):
        rows = slice(c * CHUNK, (c + 1) * CHUNK)
        cum_ref[rows, :] = _dot01(ltri, da[rows])


def _mm_dt_call(x, w, col0, dt_bias, a_log, tm, l_pad, n_pad):
    m, k = x.shape
    assert col0 % LANES == 0 and tm % CHUNK == 0 and l_pad % CHUNK == 0
    pad128 = lambda v: jnp.pad(v.astype(F32), (0, LANES - v.shape[0])).reshape(1, LANES)
    return pl.pallas_call(
        functools.partial(_mm_dt_kernel, tm=tm, l_pad=l_pad, n_pad=n_pad),
        grid=(m // tm,),
        in_specs=[
            pl.BlockSpec((tm, k), lambda i: (i, 0)),
            pl.BlockSpec((k, LANES), lambda i: (0, col0 // LANES)),
            pl.BlockSpec((1, LANES), lambda i: (0, 0)),
            pl.BlockSpec((1, LANES), lambda i: (0, 0)),
        ],
        out_specs=[
            pl.BlockSpec((tm, LANES), lambda i: (i, 0)),
            pl.BlockSpec((tm, LANES), lambda i: (i, 0)),
        ],
        out_shape=[jax.ShapeDtypeStruct((m, LANES), F32), jax.ShapeDtypeStruct((m, LANES), F32)],
        compiler_params=_cparams(("parallel",)),
        name="mm_dt",
    )(x, w, pad128(dt_bias), pad128(a_log))


MIXA_HALO = 32
MIXA_RB = 64


def _mixa_kernel(x_ref, halo_ref, w_ref, cb_ref, g_ref, b_ref, o_ref, ext_ref, conv_ref, *, tm):
    ext_ref[0:MIXA_HALO, :] = halo_ref[...]
    ext_ref[MIXA_HALO:, :] = x_ref[...]
    n_strips = D_CONV // LANES
    lead = MIXA_HALO - (CONV_A_WIDTH - 1)

    def row_block(rb, carry):
        r0 = pl.multiple_of(rb * MIXA_RB, MIXA_RB)
        s1 = jnp.zeros((MIXA_RB, LANES), F32)
        for s in range(n_strips):
            cs = slice(s * LANES, (s + 1) * LANES)
            blk = ext_ref[pl.ds(r0, MIXA_RB + MIXA_HALO), cs]
            acc = jnp.zeros((MIXA_RB, LANES), F32) + cb_ref[:, cs]
            for j in range(SUBLANES):
                sh = blk if j == 0 else blk[j:j + MIXA_RB + MIXA_HALO - SUBLANES]
                for i in range(MIXA_HALO // SUBLANES + 1):
                    kk = SUBLANES * i + j - lead
                    if 0 <= kk < CONV_A_WIDTH:
                        acc = acc + w_ref[kk:kk + 1, cs] * sh[SUBLANES * i:SUBLANES * i + MIXA_RB]
            conv_ref[:, cs] = acc
            s1 = s1 + acc
        mu = jnp.sum(s1, axis=1, keepdims=True) * (1.0 / D_CONV)
        s2 = jnp.zeros((MIXA_RB, LANES), F32)
        for s in range(n_strips):
            cs = slice(s * LANES, (s + 1) * LANES)
            d = conv_ref[:, cs] - mu
            s2 = s2 + d * d
        inv = lax.rsqrt(jnp.sum(s2, axis=1, keepdims=True) * (1.0 / D_CONV) + EPS)
        for s in range(n_strips):
            cs = slice(s * LANES, (s + 1) * LANES)
            y = (conv_ref[:, cs] - mu) * inv * g_ref[:, cs] + b_ref[:, cs]
            o_ref[pl.ds(r0, MIXA_RB), cs] = _silu(y).astype(o_ref.dtype)
        return carry

    lax.fori_loop(0, tm // MIXA_RB, row_block, 0)


def _mixa_call(a, conv_w, conv_b, ln_g, ln_b, tm):
    m, c = a.shape
    hb = tm // MIXA_HALO
    return pl.pallas_call(
        functools.partial(_mixa_kernel, tm=tm),
        grid=(m // tm,),
        in_specs=[
            pl.BlockSpec((tm, c), lambda i: (i, 0)),
            pl.BlockSpec((MIXA_HALO, c), lambda i: (jnp.maximum(i * hb - 1, 0), 0)),
            pl.BlockSpec((CONV_A_WIDTH, c), lambda i: (0, 0)),
            pl.BlockSpec((1, c), lambda i: (0, 0)),
            pl.BlockSpec((1, c), lambda i: (0, 0)),
            pl.BlockSpec((1, c), lambda i: (0, 0)),
        ],
        out_specs=pl.BlockSpec((tm, c), lambda i: (i, 0)),
        out_shape=jax.ShapeDtypeStruct((m, c), BF16),
        scratch_shapes=[
            pltpu.VMEM((tm + MIXA_HALO, c), F32),
            pltpu.VMEM((MIXA_RB, c), F32),
        ],
        compiler_params=_cparams(("parallel",)),
        name="mixer_a",
    )(a, a, conv_w, conv_b.reshape(1, c), ln_g.reshape(1, c), ln_b.reshape(1, c))


def _ssd_kernel(z_ref, xs_ref, bs_ref, cs_ref, dt_ref, cum_ref, dskip_ref, ng_ref, o_ref,
                dtb_s, ce_s, y_s, cst_s, state_s, *, tl):
    g = pl.program_id(1)
    s = pl.program_id(2)
    q = CHUNK
    gw = GROUP_W
    nch = tl // q

    hi = lax.broadcasted_iota(jnp.int32, (LANES, gw), 0)
    hj = lax.broadcasted_iota(jnp.int32, (LANES, gw), 1)
    sel = (hi == g * HEADS_PER_GROUP + hj // HEAD_DIM).astype(BF16)
    dtb_s[...] = _dot01_r(dt_ref[...], sel)
    ce_s[...] = _dot01_r(cum_ref[...], sel)

    @pl.when(s == 0)
    def _():
        state_s[...] = jnp.zeros_like(state_s)

    ri = lax.broadcasted_iota(jnp.int32, (q, gw), 0)
    ci = lax.broadcasted_iota(jnp.int32, (q, gw), 1)
    tri = ri >= (ci % q)
    eye = ri == (ci % q)
    half = gw // 2
    bi = lax.broadcasted_iota(jnp.int32, (half, half), 0)
    bj = lax.broadcasted_iota(jnp.int32, (half, half), 1)
    bdiag = (bi // q) == (bj // HEAD_DIM)

    for c in range(nch):
        rows = slice(c * q, (c + 1) * q)
        xs_c = xs_ref[rows, :]
        bs_b = bs_ref[rows, :]
        cs_b = cs_ref[rows, :]
        ce = ce_s[rows, :]
        rr = jnp.sum(jnp.where(eye, ce, 0.0), axis=0, keepdims=True)
        decay = jnp.where(tri, jnp.exp(jnp.where(tri, ce - rr, 0.0)), 0.0)
        bs_rep = jnp.concatenate([bs_b] * HEADS_PER_GROUP, axis=0)
        cbt = lax.dot_general(cs_b, bs_rep, (((1,), (1,)), ((), ())),
                              preferred_element_type=F32)
        mm = (cbt * decay).astype(BF16)
        xdt_c = xs_c * dtb_s[rows, :]
        xdt_b = xdt_c.astype(BF16)
        ys = []
        for hh in range(2):
            blk = xdt_b[:, hh * half:(hh + 1) * half]
            xd = jnp.where(bdiag, jnp.concatenate([blk] * (half // q), axis=0), jnp.zeros((), BF16))
            ys.append(jnp.dot(mm[:, hh * half:(hh + 1) * half], xd, preferred_element_type=F32))
        y_s[rows, :] = jnp.concatenate(ys, axis=1) + dskip_ref[...] * xs_c
        last = ce[q - 1:q, :]
        xw = (xdt_c * jnp.exp(last - ce)).astype(BF16)
        bs_t = bs_b.astype(F32).T.astype(BF16)
        cst_s[c] = jnp.dot(bs_t, xw, preferred_element_type=F32)

    for c in range(nch):
        rows = slice(c * q, (c + 1) * q)
        ce = ce_s[rows, :]
        st = state_s[...]
        y = y_s[rows, :] + jnp.dot(cs_ref[rows, :], st.astype(BF16),
                                   preferred_element_type=F32) * jnp.exp(ce)
        state_s[...] = st * jnp.exp(ce[q - 1:q, :]) + cst_s[c]
        u = y * _silu(z_ref[rows, :])
        ms = jnp.mean(u * u, axis=-1, keepdims=True)
        o_ref[rows, :] = (u * lax.rsqrt(ms + EPS) * ng_ref[...]).astype(o_ref.dtype)


def _ssd_call(z, xs, bc, dt, cum, d_skip, norm_g, bsz, l_pad, tl):
    m = z.shape[0]
    spb = l_pad // tl
    gw = GROUP_W
    n = D_STATE
    assert l_pad % tl == 0 and tl % CHUNK == 0

    def row(b, g, s):
        return b * spb + s

    return pl.pallas_call(
        functools.partial(_ssd_kernel, tl=tl),
        grid=(bsz, N_GROUPS, spb),
        in_specs=[
            pl.BlockSpec((tl, gw), lambda b, g, s: (row(b, g, s), g)),
            pl.BlockSpec((tl, gw), lambda b, g, s: (row(b, g, s), g)),
            pl.BlockSpec((tl, n), lambda b, g, s: (row(b, g, s), g)),
            pl.BlockSpec((tl, n), lambda b, g, s: (row(b, g, s), N_GROUPS + g)),
            pl.BlockSpec((tl, LANES), lambda b, g, s: (row(b, g, s), 0)),
            pl.BlockSpec((tl, LANES), lambda b, g, s: (row(b, g, s), 0)),
            pl.BlockSpec((1, gw), lambda b, g, s: (0, g)),
            pl.BlockSpec((1, gw), lambda b, g, s: (0, g)),
        ],
        out_specs=pl.BlockSpec((tl, gw), lambda b, g, s: (row(b, g, s), g)),
        out_shape=jax.ShapeDtypeStruct((m, D_INNER), BF16),
        scratch_shapes=[
            pltpu.VMEM((tl, gw), F32),
            pltpu.VMEM((tl, gw), F32),
            pltpu.VMEM((tl, gw), F32),
            pltpu.VMEM((tl // CHUNK, n, gw), F32),
            pltpu.VMEM((n, gw), F32),
        ],
        compiler_params=_cparams(("parallel", "parallel", "arbitrary")),
        name="ssd",
    )(z, xs, bc, bc, dt, cum,
      jnp.repeat(d_skip.astype(F32), HEAD_DIM).reshape(1, D_INNER),
      norm_g.reshape(1, D_INNER))


def _merge_kernel(a_ref, y_ref, ga_ref, gb_ref, wa_ref, wb_ref, o_ref):
    ya = jnp.dot(a_ref[...], wa_ref[...], preferred_element_type=F32)
    yb = jnp.dot(y_ref[...], wb_ref[...], preferred_element_type=F32)
    o_ref[...] = (_sigmoid(ga_ref[...]) * ya + _sigmoid(gb_ref[...]) * yb).astype(o_ref.dtype)


def _merge_call(a_act, y_norm, gates, w_a, w_b, tm, tn):
    m = a_act.shape[0]
    nb = D_MODEL // tn
    return pl.pallas_call(
        _merge_kernel,
        grid=(m // tm, nb),
        in_specs=[
            pl.BlockSpec((tm, D_CONV), lambda i, j: (i, 0)),
            pl.BlockSpec((tm, D_INNER), lambda i, j: (i, 0)),
            pl.BlockSpec((tm, tn), lambda i, j: (i, j)),
            pl.BlockSpec((tm, tn), lambda i, j: (i, nb + j)),
            pl.BlockSpec((D_CONV, tn), lambda i, j: (0, j)),
            pl.BlockSpec((D_INNER, tn), lambda i, j: (0, j)),
        ],
        out_specs=pl.BlockSpec((tm, tn), lambda i, j: (i, j)),
        out_shape=jax.ShapeDtypeStruct((m, D_MODEL), BF16),
        compiler_params=_cparams(("parallel", "arbitrary")),
        name="merge",
    )(a_act, y_norm, gates, gates, w_a, w_b)


LN_RB = 176


def _mm_ln_kernel(x_ref, w_ref, res_ref, g_ref, b_ref, of_ref, ob_ref, acc_ref, *, nk, tm, l_pad, n_pad):
    k = pl.program_id(1)

    @pl.when(k == 0)
    def _():
        acc_ref[...] = jnp.zeros_like(acc_ref)

    acc_ref[...] += jnp.dot(x_ref[...], w_ref[...], preferred_element_type=F32)

    @pl.when(k == nk - 1)
    def _():
        for r in range(tm // LN_RB):
            rows = slice(r * LN_RB, (r + 1) * LN_RB)
            v = ALPHA * res_ref[rows, :] + acc_ref[rows, :]
            y = _layer_norm_rows(v, g_ref[...], b_ref[...])
            real = _pad_row_mask(pl.program_id(0) * tm + r * LN_RB, LN_RB, l_pad, n_pad)
            y = jnp.where(real, y, 0.0)
            of_ref[rows, :] = y
            ob_ref[rows, :] = y.astype(BF16)


def _mm_ln_call(x, w, res, g, b, tm, tk, l_pad, n_pad, name):
    m, kdim = x.shape
    d = w.shape[1]
    nk = kdim // tk
    assert m % tm == 0 and kdim % tk == 0 and tm % LN_RB == 0
    return pl.pallas_call(
        functools.partial(_mm_ln_kernel, nk=nk, tm=tm, l_pad=l_pad, n_pad=n_pad),
        grid=(m // tm, nk),
        in_specs=[
            pl.BlockSpec((tm, tk), lambda i, k: (i, k)),
            pl.BlockSpec((tk, d), lambda i, k: (k, 0)),
            pl.BlockSpec((tm, d), lambda i, k: (i, 0)),
            pl.BlockSpec((1, d), lambda i, k: (0, 0)),
            pl.BlockSpec((1, d), lambda i, k: (0, 0)),
        ],
        out_specs=[
            pl.BlockSpec((tm, d), lambda i, k: (i, 0)),
            pl.BlockSpec((tm, d), lambda i, k: (i, 0)),
        ],
        out_shape=[jax.ShapeDtypeStruct((m, d), F32), jax.ShapeDtypeStruct((m, d), BF16)],
        scratch_shapes=[pltpu.VMEM((tm, d), F32)],
        compiler_params=_cparams(("parallel", "arbitrary")),
        name=name,
    )(x, w, res, g.reshape(1, d), b.reshape(1, d))


def _ffn_up_kernel(x_ref, xh_ref, w1_ref, w2_ref, cw1_ref, cw2_ref, cb1_ref, cb2_ref, o_ref, *, tm):
    x = x_ref[...]
    xh = xh_ref[...]

    def conv(w_ref, cw_ref, cb_ref):
        w = w_ref[...]
        u_main = jnp.dot(x, w, preferred_element_type=F32)
        u_halo = jnp.dot(xh, w, preferred_element_type=F32)
        return _causal_taps(u_halo, u_main, cw_ref, cb_ref, CONV_F_WIDTH, tm)

    u1 = conv(w1_ref, cw1_ref, cb1_ref)
    u2 = conv(w2_ref, cw2_ref, cb2_ref)
    o_ref[...] = (_silu(u1) * u2).astype(o_ref.dtype)


def _ffn_up_call(hb, w_up, conv_w, conv_b, tm, tn):
    m, k = hb.shape
    nb = D_FF // tn
    hpt = tm // BF16_ROWS
    cb = conv_b.reshape(1, 2 * D_FF)
    return pl.pallas_call(
        functools.partial(_ffn_up_kernel, tm=tm),
        grid=(m // tm, nb),
        in_specs=[
            pl.BlockSpec((tm, k), lambda i, j: (i, 0)),
            pl.BlockSpec((BF16_ROWS, k), lambda i, j: (jnp.maximum(i * hpt - 1, 0), 0)),
            pl.BlockSpec((k, tn), lambda i, j: (0, j)),
            pl.BlockSpec((k, tn), lambda i, j: (0, nb + j)),
            pl.BlockSpec((CONV_F_WIDTH, tn), lambda i, j: (0, j)),
            pl.BlockSpec((CONV_F_WIDTH, tn), lambda i, j: (0, nb + j)),
            pl.BlockSpec((1, tn), lambda i, j: (0, j)),
            pl.BlockSpec((1, tn), lambda i, j: (0, nb + j)),
        ],
        out_specs=pl.BlockSpec((tm, tn), lambda i, j: (i, j)),
        out_shape=jax.ShapeDtypeStruct((m, D_FF), BF16),
        compiler_params=_cparams(("parallel", "arbitrary")),
        name="ffn_up",
    )(hb, hb, w_up, w_up, conv_w, conv_w, cb, cb)


def kernel(x, meta_tokens, ln_in_g, ln_in_b, w_in, conv_a_w, conv_a_b, ln_a_g, ln_a_b, w_a_out,
           conv_b_w, conv_b_b, dt_bias, a_log, d_skip, norm_b_g, w_b_out, w_o, ln1_g, ln1_b,
           w_up, conv_f_w, conv_f_b, w_down, ln2_g, ln2_b):
    bsz, seq, d = x.shape
    assert d == D_MODEL and seq % CHUNK == 0 and N_META <= CHUNK
    n_pad = CHUNK - N_META
    l_pad = seq + CHUNK
    m = bsz * l_pad

    tm = 1056
    tl = 704
    assert m % tm == 0 and l_pad % tl == 0 and tl % CHUNK == 0

    hf, hb = _ln_in_call(x, meta_tokens, ln_in_g, ln_in_b, l_pad, n_pad)

    for i in range(DEPTH):
        w_in_b = w_in[i].astype(BF16)
        w_g_b = w_in[i][:, OFF_GATES:].astype(BF16)

        a_glu = _mm_glu_call(hb, w_in_b, D_CONV, tm, 512, F32)
        z = _mm_call(hb, w_in_b, OFF_Z, D_INNER, tm, 1024, F32, "mm_z")
        xs = _mm_conv_call(hb, w_in_b, OFF_XBC, D_INNER, conv_b_w[i], conv_b_b[i], 0,
                           tm, 512, F32, "mm_xs")
        bc = _mm_conv_call(hb, w_in_b, OFF_BC, 2 * N_GROUPS * D_STATE, conv_b_w[i], conv_b_b[i], D_INNER,
                           tm, 512, BF16, "mm_bc")
        dt, cum = _mm_dt_call(hb, w_in_b, OFF_DT, dt_bias[i], a_log[i], tl, l_pad, n_pad)
        gates = _mm_call(hb, w_g_b, 0, 2 * D_MODEL, tm, 1024, F32, "mm_gates")

        a_act = _mixa_call(a_glu, conv_a_w[i], conv_a_b[i], ln_a_g[i], ln_a_b[i], tl)
        y_norm = _ssd_call(z, xs, bc, dt, cum, d_skip[i], norm_b_g[i], bsz, l_pad, tl)

        merged = _merge_call(a_act, y_norm, gates, w_a_out[i].astype(BF16), w_b_out[i].astype(BF16),
                             tl, 512)
        hf, hb = _mm_ln_call(merged, w_o[i].astype(BF16), hf, ln1_g[i], ln1_b[i],
                             tl, 1024, l_pad, n_pad, "mm_o_ln")

        act = _ffn_up_call(hb, w_up[i].astype(BF16), conv_f_w[i], conv_f_b[i], tm, 512)
        hf, hb = _mm_ln_call(act, w_down[i].astype(BF16), hf, ln2_g[i], ln2_b[i],
                             tl, 512, l_pad, n_pad, "mm_down_ln")

    return hf.reshape(bsz, l_pad, d)[:, CHUNK:, :]
```
